```python
import jax, jax.numpy as jnp
from jax import lax
import numpy as np

D_MODEL = 4096
BATCH = 2
SEQ = 8192
DEPTH = 4

N_MIXERS = 2
N_RWKV = (DEPTH + 1) // 2
N_GLA = DEPTH // 2

ALPHA = (2.0 * DEPTH) ** 0.25
BETA = (8.0 * DEPTH) ** -0.25
LN_EPS = 1e-5

RWKV_HEAD = 64
RWKV_HEADS = D_MODEL // RWKV_HEAD

def _lora_dim(mult, power):
    return max(32, int(round(D_MODEL ** power * mult / 32.0)) * 32)

LORA_W = _lora_dim(1.8, 0.5)
LORA_A = _lora_dim(1.8, 0.5)
LORA_V = _lora_dim(1.3, 0.5)
LORA_G = _lora_dim(0.8, 0.6)
RWKV_GN_EPS = 64e-5

GLA_HEADS = D_MODEL // 512
GLA_DK_TOT = D_MODEL // 2
GLA_DV_TOT = D_MODEL
GLA_DK = GLA_DK_TOT // GLA_HEADS
GLA_DV = GLA_DV_TOT // GLA_HEADS
GLA_GATE_RANK = 16
GLA_GATE_NORM = 16.0
GLA_CHUNK = 64
GLA_IN_DIM = 2 * GLA_DK_TOT + 2 * GLA_DV_TOT + GLA_GATE_RANK
GLA_RMS_EPS = 1e-5

N_GROUPS = 4
EXPERTS_PER_GROUP = 4
N_EXPERTS = N_GROUPS * EXPERTS_PER_GROUP
TOP_K = 2
D_EXPERT = D_MODEL // 8

kernel_name = 'rwkv7_gla_interleaved_grouped_moe_deepnorm'


def _layernorm(x, g, b):
    xf = x.astype(jnp.float32)
    mu = jnp.mean(xf, -1, keepdims=True)
    var = jnp.mean(jnp.square(xf - mu), -1, keepdims=True)
    return ((xf - mu) * lax.rsqrt(var + LN_EPS) * g + b).astype(x.dtype)


def rwkv7_mix(x, v_first, mix, w_r, w_k, w_v, w_o, w0, w1, w2, a0, a1, a2,
              g1, g2, k_k, k_a, r_k, lnx_w, lnx_b, vres):
    B, S, D = x.shape
    H, N = RWKV_HEADS, RWKV_HEAD
    x_prev = jnp.pad(x[:, :-1], ((0, 0), (1, 0), (0, 0)))
    xx = x_prev - x
    xr, xw, xk, xv, xa, xg = [x + xx * mix[i] for i in range(6)]
    r = xr @ w_r
    k = xk @ w_k
    v = xv @ w_v
    w = -jax.nn.softplus(-(w0 + jnp.tanh(xw @ w1) @ w2)) - 0.5
    a = jax.nn.sigmoid(a0 + (xa @ a1) @ a2)
    g = jax.nn.sigmoid(xg @ g1) @ g2
    if vres is None:
        v_first = v
    else:
        v0, v1, v2 = vres
        v = v + (v_first - v) * jax.nn.sigmoid(v0 + (xv @ v1) @ v2)

    heads = lambda t: t.reshape(B, S, H, N).astype(jnp.float32)
    kk = heads(k * k_k)
    kk = kk / jnp.maximum(jnp.sqrt(jnp.sum(kk * kk, -1, keepdims=True)), 1e-12)
    a_h = heads(a)
    k_h = heads(k) * (1.0 + (a_h - 1.0) * k_a.reshape(H, N).astype(jnp.float32))
    r_h, v_h = heads(r), heads(v)
    decay = jnp.exp(-jnp.exp(heads(w)))

    tm = lambda t: jnp.moveaxis(t, 1, 0)
    xs = (tm(r_h), tm(decay), tm(k_h), tm(v_h), tm(-kk), tm(kk * a_h))

    def step(state, inp):
        r_t, w_t, k_t, v_t, a_t, b_t = inp
        sa = jnp.einsum('bhvk,bhk->bhv', state, a_t)
        state = (state * w_t[:, :, None, :] + sa[..., None] * b_t[:, :, None, :]
                 + v_t[..., None] * k_t[:, :, None, :])
        return state, jnp.einsum('bhvk,bhk->bhv', state, r_t)

    state0 = jnp.zeros((B, H, N, N), jnp.float32)
    _, ys = lax.scan(step, state0, xs)
    y = jnp.moveaxis(ys, 0, 1)
    mu = jnp.mean(y, -1, keepdims=True)
    var = jnp.mean(jnp.square(y - mu), -1, keepdims=True)
    y = (y - mu) * lax.rsqrt(var + RWKV_GN_EPS) * lnx_w.reshape(H, N) + lnx_b.reshape(H, N)
    bonus = jnp.sum(r_h * k_h * r_k, -1, keepdims=True) * v_h
    out = (y + bonus).reshape(B, S, D).astype(x.dtype) * g
    return out @ w_o, v_first


def gla_mix(x, w_in, gk_w2, gk_b, norm_w, w_o):
    B, S, D = x.shape
    H, DK, DV, C = GLA_HEADS, GLA_DK, GLA_DV, GLA_CHUNK
    NC = S // C
    proj = x @ w_in
    q, k, v, g, gk_lr = jnp.split(
        proj, [GLA_DK_TOT, 2 * GLA_DK_TOT, 2 * GLA_DK_TOT + GLA_DV_TOT,
               2 * GLA_DK_TOT + 2 * GLA_DV_TOT], axis=-1)
    gk = jax.nn.log_sigmoid((gk_lr @ gk_w2 + gk_b).astype(jnp.float32)) / GLA_GATE_NORM

    def chunks(t, d):
        return t.astype(jnp.float32).reshape(B, NC, C, H, d).transpose(1, 0, 3, 2, 4)

    xs = (chunks(q, DK) * DK ** -0.5, chunks(k, DK), chunks(v, DV), chunks(gk, DK))
    mask = jnp.tril(jnp.ones((C, C), dtype=bool))

    def step(state, inp):
        q_c, k_c, v_c, g_c = inp
        b = jnp.cumsum(g_c, axis=-2)
        b_last = b[..., -1:, :]
        q_e = q_c * jnp.exp(b)
        k_e = k_c * jnp.exp(-b)
        att = jnp.where(mask, jnp.einsum('bhid,bhjd->bhij', q_e, k_e), 0.0)
        o = att @ v_c + jnp.einsum('bhid,bhdv->bhiv', q_e, state)
        state = (state * jnp.swapaxes(jnp.exp(b_last), -1, -2)
                 + jnp.einsum('bhjd,bhjv->bhdv', k_c * jnp.exp(b_last - b), v_c))
        return state, o

    state0 = jnp.zeros((B, H, DK, DV), jnp.float32)
    _, os_ = lax.scan(step, state0, xs)
    o = os_.transpose(1, 0, 3, 2, 4).reshape(B, S, H, DV)
    o = o * lax.rsqrt(jnp.mean(o * o, -1, keepdims=True) + GLA_RMS_EPS) * norm_w
    o = o.reshape(B, S, GLA_DV_TOT).astype(x.dtype) * jax.nn.silu(g)
    return o @ w_o


def grouped_moe(x, router_w, router_bias, w_gate, w_up, w_down):
    B, S, D = x.shape
    xt = x.reshape(-1, D)
    T = xt.shape[0]
    scores = jax.nn.sigmoid((xt @ router_w).astype(jnp.float32))
    biased = scores + router_bias
    grouped = biased.reshape(T, N_GROUPS, EXPERTS_PER_GROUP)
    group_score = jnp.sum(lax.top_k(grouped, 2)[0], -1)
    best_group = jnp.argmax(group_score, -1)
    in_group = (jnp.arange(N_EXPERTS) // EXPERTS_PER_GROUP)[None, :] == best_group[:, None]
    masked = jnp.where(in_group, biased, -jnp.inf)
    _, idx = lax.top_k(masked, TOP_K)
    sel = jnp.take_along_axis(scores, idx, -1)
    weights = sel / jnp.sum(sel, -1, keepdims=True)
    combine = jnp.sum(jax.nn.one_hot(idx, N_EXPERTS, dtype=jnp.float32) * weights[..., None], 1)
    h = jax.nn.silu(jnp.einsum('td,edf->tef', xt, w_gate)) * jnp.einsum('td,edf->tef', xt, w_up)
    y = jnp.einsum('tef,efd->td', h * combine[..., None].astype(h.dtype), w_down)
    return y.reshape(B, S, D)


def setup_inputs(seed: int = 0) -> dict:
    key = jax.random.key(seed)
    ks = iter(jax.random.split(key, 40))
    f32 = jnp.float32
    nrm = lambda shape, scale: jax.random.normal(next(ks), shape, f32) * scale
    D, NR, NG, H, N = D_MODEL, N_RWKV, N_GLA, RWKV_HEADS, RWKV_HEAD
    s_d = D ** -0.5
    return {
        'x': nrm((BATCH, SEQ, D), 1.0),
        'rwkv_mix': jax.random.uniform(next(ks), (NR, 6, D), f32),
        'rwkv_w_r': nrm((NR, D, D), s_d),
        'rwkv_w_k': nrm((NR, D, D), s_d),
        'rwkv_w_v': nrm((NR, D, D), s_d),
        'rwkv_w_o': nrm((NR, D, D), s_d * BETA),
        'rwkv_w0': jax.random.uniform(next(ks), (NR, D), f32, -6.5, -1.5),
        'rwkv_w1': nrm((NR, D, LORA_W), s_d),
        'rwkv_w2': nrm((NR, LORA_W, D), 0.1 * LORA_W ** -0.5),
        'rwkv_a0': nrm((NR, D), 0.1),
        'rwkv_a1': nrm((NR, D, LORA_A), s_d),
        'rwkv_a2': nrm((NR, LORA_A, D), 0.5 * LORA_A ** -0.5),
        'rwkv_v0': 1.0 + nrm((NR - 1, D), 0.1),
        'rwkv_v1': nrm((NR - 1, D, LORA_V), s_d),
        'rwkv_v2': nrm((NR - 1, LORA_V, D), 0.5 * LORA_V ** -0.5),
        'rwkv_g1': nrm((NR, D, LORA_G), s_d),
        'rwkv_g2': nrm((NR, LORA_G, D), LORA_G ** -0.5),
        'rwkv_k_k': 0.85 + nrm((NR, D), 0.05),
        'rwkv_k_a': 1.0 + nrm((NR, D), 0.05),
        'rwkv_r_k': nrm((NR, H, N), 0.1),
        'rwkv_lnx_w': 1.0 + nrm((NR, D), 0.05),
        'rwkv_lnx_b': nrm((NR, D), 0.01),
        'gla_w_in': nrm((NG, D, GLA_IN_DIM), s_d),
        'gla_gk_w2': nrm((NG, GLA_GATE_RANK, GLA_DK_TOT), GLA_GATE_RANK ** -0.5),
        'gla_gk_b': nrm((NG, GLA_DK_TOT), 0.1),
        'gla_norm_w': 1.0 + nrm((NG, GLA_DV), 0.05),
        'gla_w_o': nrm((NG, GLA_DV_TOT, D), GLA_DV_TOT ** -0.5 * BETA),
        'ln_g': 1.0 + nrm((DEPTH, 2, D), 0.05),
        'ln_b': nrm((DEPTH, 2, D), 0.01),
        'moe_w_gate': nrm((DEPTH, N_EXPERTS, D, D_EXPERT), s_d),
        'moe_w_up': nrm((DEPTH, N_EXPERTS, D, D_EXPERT), s_d),
        'moe_w_down': nrm((DEPTH, N_EXPERTS, D_EXPERT, D), D_EXPERT ** -0.5 * BETA),
        'router_w': nrm((D, N_EXPERTS), s_d),
        'router_bias': nrm((N_EXPERTS,), 0.01),
    }


def reference(x, rwkv_mix, rwkv_w_r, rwkv_w_k, rwkv_w_v, rwkv_w_o, rwkv_w0, rwkv_w1, rwkv_w2,
              rwkv_a0, rwkv_a1, rwkv_a2, rwkv_v0, rwkv_v1, rwkv_v2, rwkv_g1, rwkv_g2,
              rwkv_k_k, rwkv_k_a, rwkv_r_k, rwkv_lnx_w, rwkv_lnx_b,
              gla_w_in, gla_gk_w2, gla_gk_b, gla_norm_w, gla_w_o,
              ln_g, ln_b, moe_w_gate, moe_w_up, moe_w_down, router_w, router_bias):
    v_first = None
    for i in range(DEPTH):
        j = i // N_MIXERS
        if i % N_MIXERS == 0:
            vres = None if j == 0 else (rwkv_v0[j - 1], rwkv_v1[j - 1], rwkv_v2[j - 1])
            y, v_first = rwkv7_mix(
                x, v_first, rwkv_mix[j], rwkv_w_r[j], rwkv_w_k[j], rwkv_w_v[j], rwkv_w_o[j],
                rwkv_w0[j], rwkv_w1[j], rwkv_w2[j], rwkv_a0[j], rwkv_a1[j], rwkv_a2[j],
                rwkv_g1[j], rwkv_g2[j], rwkv_k_k[j], rwkv_k_a[j], rwkv_r_k[j],
                rwkv_lnx_w[j], rwkv_lnx_b[j], vres)
        else:
            y = gla_mix(x, gla_w_in[j], gla_gk_w2[j], gla_gk_b[j], gla_norm_w[j], gla_w_o[j])
        x = _layernorm(ALPHA * x + y, ln_g[i, 0], ln_b[i, 0])
        m = grouped_moe(x, router_w, router_bias, moe_w_gate[i], moe_w_up[i], moe_w_down[i])
        x = _layernorm(ALPHA * x + m, ln_g[i, 1], ln_b[i, 1])
    return x
```

```python
import functools

import jax
import jax.numpy as jnp
from jax import lax
from jax.experimental import pallas as pl
from jax.experimental.pallas import tpu as pltpu

F32 = jnp.float32
BF16 = jnp.bfloat16
HIGHEST = lax.Precision.HIGHEST

LN_EPS = 1e-5
RWKV_HEAD = 64
RWKV_GN_EPS = 64e-5
RWKV_CHUNK = 64
GLA_HEAD_DV = 512
GLA_CHUNK = 64
GLA_GATE_NORM = 16.0
GLA_RMS_EPS = 1e-5
N_GROUPS = 4
EXPERTS_PER_GROUP = 4

V7X_VMEM_REQUEST_CAP = 56 * 1024 * 1024


def _compiler_params(semantics, vmem_bytes):
    limit = int(min(max(vmem_bytes * 5 // 4 + (4 << 20), 32 << 20), V7X_VMEM_REQUEST_CAP))
    return pltpu.CompilerParams(dimension_semantics=semantics, vmem_limit_bytes=limit)


def _mm_kernel(x_ref, w_ref, o_ref):
    o_ref[...] = jnp.dot(x_ref[...], w_ref[...], preferred_element_type=F32).astype(o_ref.dtype)


def _mm_acc_kernel(x_ref, w_ref, o_ref, acc_ref):
    k = pl.program_id(2)

    @pl.when(k == 0)
    def _():
        acc_ref[...] = jnp.zeros_like(acc_ref)

    acc_ref[...] += jnp.dot(x_ref[...], w_ref[...], preferred_element_type=F32)

    @pl.when(k == pl.num_programs(2) - 1)
    def _():
        o_ref[...] = acc_ref[...].astype(o_ref.dtype)


def _matmul(x, w, out_dtype, tm=1024, tn=1024, tk=None):
    M, K = x.shape
    K2, N = w.shape
    assert K == K2
    tm, tn = min(tm, M), min(tn, N)
    tk = K if tk is None else min(tk, K)
    assert M % tm == 0 and N % tn == 0 and K % tk == 0
    nk = K // tk
    ob = jnp.dtype(out_dtype).itemsize
    vmem = 2 * (tm * tk * 2 + tk * tn * 2 + tm * tn * ob) + (tm * tn * 4 if nk > 1 else 0)
    if nk == 1:
        return pl.pallas_call(
            _mm_kernel,
            grid=(M // tm, N // tn),
            in_specs=[pl.BlockSpec((tm, K), lambda i, j: (i, 0)),
                      pl.BlockSpec((K, tn), lambda i, j: (0, j))],
            out_specs=pl.BlockSpec((tm, tn), lambda i, j: (i, j)),
            out_shape=jax.ShapeDtypeStruct((M, N), out_dtype),
            compiler_params=_compiler_params(("parallel", "arbitrary"), vmem),
            name="matmul",
        )(x, w)
    return pl.pallas_call(
        _mm_acc_kernel,
        grid=(M // tm, N // tn, nk),
        in_specs=[pl.BlockSpec((tm, tk), lambda i, j, k: (i, k)),
                  pl.BlockSpec((tk, tn), lambda i, j, k: (k, j))],
        out_specs=pl.BlockSpec((tm, tn), lambda i, j, k: (i, j)),
        out_shape=jax.ShapeDtypeStruct((M, N), out_dtype),
        scratch_shapes=[pltpu.VMEM((tm, tn), F32)],
        compiler_params=_compiler_params(("parallel", "arbitrary", "arbitrary"), vmem),
        name="matmul_ksplit",
    )(x, w)


def _ln_kernel(x_ref, y_ref, g_ref, b_ref, o_ref, ob_ref, *, alpha):
    z = alpha * x_ref[...] + y_ref[...].astype(F32)
    mu = jnp.mean(z, axis=-1, keepdims=True)
    zc = z - mu
    var = jnp.mean(zc * zc, axis=-1, keepdims=True)
    o = zc * lax.rsqrt(var + LN_EPS) * g_ref[...] + b_ref[...]
    o_ref[...] = o
    ob_ref[...] = o.astype(BF16)


def _add_layernorm(x, y, g, b, alpha, tm=256):
    T, D = x.shape
    tm = min(tm, T)
    assert T % tm == 0
    vmem = 2 * tm * D * (4 + y.dtype.itemsize + 4 + 2)
    row = pl.BlockSpec((tm, D), lambda i: (i, 0))
    vec = pl.BlockSpec((1, D), lambda i: (0, 0))
    return pl.pallas_call(
        functools.partial(_ln_kernel, alpha=alpha),
        grid=(T // tm,),
        in_specs=[row, row, vec, vec],
        out_specs=[row, row],
        out_shape=[jax.ShapeDtypeStruct((T, D), F32), jax.ShapeDtypeStruct((T, D), BF16)],
        compiler_params=_compiler_params(("parallel",), vmem),
        name="add_layernorm",
    )(x, y, g.reshape(1, D), b.reshape(1, D))


def _top2_sum(a, b, c, d):
    s1, t1 = jnp.maximum(a, b), jnp.minimum(a, b)
    s2, t2 = jnp.maximum(c, d), jnp.minimum(c, d)
    return jnp.maximum(s1, s2) + jnp.maximum(jnp.minimum(s1, s2), jnp.maximum(t1, t2))


def _router_kernel(x_ref, rw_ref, bias_ref, comb_ref, *, n_experts):
    logits = jnp.dot(x_ref[...], rw_ref[...], precision=HIGHEST, preferred_element_type=F32)
    scores = jax.nn.sigmoid(logits)
    biased = scores + bias_ref[...]
    sc = [scores[:, e:e + 1] for e in range(n_experts)]
    bi = [biased[:, e:e + 1] for e in range(n_experts)]
    epg = EXPERTS_PER_GROUP
    group_score = [_top2_sum(*bi[epg * g:epg * (g + 1)]) for g in range(N_GROUPS)]
    best = group_score[0]
    bg = jnp.zeros(best.shape, jnp.int32)
    for g in range(1, N_GROUPS):
        upd = group_score[g] > best
        best = jnp.where(upd, group_score[g], best)
        bg = jnp.where(upd, g, bg)
    vb = list(bi[:epg])
    vs = list(sc[:epg])
    for g in range(1, N_GROUPS):
        m = bg == g
        for j in range(epg):
            vb[j] = jnp.where(m, bi[epg * g + j], vb[j])
            vs[j] = jnp.where(m, sc[epg * g + j], vs[j])
    m1, s1 = vb[0], vs[0]
    i1 = jnp.zeros(best.shape, jnp.int32)
    for j in range(1, epg):
        upd = vb[j] > m1
        m1 = jnp.where(upd, vb[j], m1)
        s1 = jnp.where(upd, vs[j], s1)
        i1 = jnp.where(upd, j, i1)
    m2 = jnp.full(best.shape, -jnp.inf, F32)
    s2 = jnp.zeros(best.shape, F32)
    i2 = jnp.zeros(best.shape, jnp.int32)
    for j in range(epg):
        upd = (i1 != j) & (vb[j] > m2)
        m2 = jnp.where(upd, vb[j], m2)
        s2 = jnp.where(upd, vs[j], s2)
        i2 = jnp.where(upd, j, i2)
    denom = s1 + s2
    e1 = epg * bg + i1
    e2 = epg * bg + i2
    lane = lax.broadcasted_iota(jnp.int32, logits.shape, 1)
    comb = jnp.where(lane == e1, s1 / denom, 0.0) + jnp.where(lane == e2, s2 / denom, 0.0)
    comb_ref[...] = comb[:, :n_experts]


def _router(x, router_w, router_bias, tm=256):
    T, D = x.shape
    E = router_w.shape[1]
    assert E == N_GROUPS * EXPERTS_PER_GROUP
    lanes = 128
    rw = jnp.zeros((D, lanes), F32).at[:, :E].set(router_w)
    bias = jnp.zeros((1, lanes), F32).at[0, :E].set(router_bias)
    tm = min(tm, T)
    assert T % tm == 0
    vmem = 2 * (tm * D * 4 + D * lanes * 4) + 8 * tm * lanes * 4
    return pl.pallas_call(
        functools.partial(_router_kernel, n_experts=E),
        grid=(T // tm,),
        in_specs=[pl.BlockSpec((tm, D), lambda i: (i, 0)),
                  pl.BlockSpec((D, lanes), lambda i: (0, 0)),
                  pl.BlockSpec((1, lanes), lambda i: (0, 0))],
        out_specs=pl.BlockSpec((tm, E), lambda i: (i, 0)),
        out_shape=jax.ShapeDtypeStruct((T, E), F32),
        compiler_params=_compiler_params(("parallel",), vmem),
        name="router",
    )(x, rw, bias)


def _moe_up_kernel(x_ref, wg_ref, wu_ref, comb_ref, h_ref):
    x = x_ref[...]
    g = jnp.dot(x, wg_ref[0], preferred_element_type=F32)
    u = jnp.dot(x, wu_ref[0], preferred_element_type=F32)
    comb = comb_ref[...]
    lane = lax.broadcasted_iota(jnp.int32, comb.shape, 1)
    c = jnp.sum(jnp.where(lane == pl.program_id(1), comb, 0.0), axis=1, keepdims=True)
    h_ref[...] = (g * jax.nn.sigmoid(g) * u * c).astype(h_ref.dtype)


def _moe_up(x, w_gate, w_up, comb, tm=1024):
    T, D = x.shape
    E, _, F = w_gate.shape
    tm = min(tm, T)
    assert T % tm == 0
    vmem = 2 * (tm * D * 2 + 2 * D * F * 2 + tm * F * 2 + tm * 128 * 4) + 3 * tm * F * 4
    return pl.pallas_call(
        _moe_up_kernel,
        grid=(T // tm, E),
        in_specs=[pl.BlockSpec((tm, D), lambda i, e: (i, 0)),
                  pl.BlockSpec((1, D, F), lambda i, e: (e, 0, 0)),
                  pl.BlockSpec((1, D, F), lambda i, e: (e, 0, 0)),
                  pl.BlockSpec((tm, E), lambda i, e: (i, 0))],
        out_specs=pl.BlockSpec((tm, F), lambda i, e: (i, e)),
        out_shape=jax.ShapeDtypeStruct((T, E * F), BF16),
        compiler_params=_compiler_params(("parallel", "arbitrary"), vmem),
        name="moe_up",
    )(x, w_gate, w_up, comb)


def _grouped_moe(x_f32, x_bf16, router_w, router_bias, w_gate, w_up, w_down):
    E, F, D = w_down.shape
    comb = _router(x_f32, router_w, router_bias)
    h = _moe_up(x_bf16, w_gate.astype(BF16), w_up.astype(BF16), comb)
    return _matmul(h, w_down.astype(BF16).reshape(E * F, D), F32, tk=2048)


_NN = (((2,), (1,)), ((0,), (0,)))
_NT = (((2,), (2,)), ((0,), (0,)))
_TN = (((1,), (1,)), ((0,), (0,)))


def _bdot(a, b, dims, precision=None):
    if precision is None:
        a, b = a.astype(BF16), b.astype(BF16)
    return lax.dot_general(a, b, dims, precision=precision, preferred_element_type=F32)


def _rwkv_scan_kernel(r_ref, w_ref, k_ref, v_ref, a_ref, kk_ref, ka_ref, rk_ref, gw_ref, gb_ref,
                      o_ref, s_ref):
    @pl.when(pl.program_id(1) == 0)
    def _():
        s_ref[...] = jnp.zeros_like(s_ref)

    r = r_ref[...]
    k = k_ref[...]
    v = v_ref[...]
    asig = a_ref[...]
    hb, C, N = r.shape
    lw = -jnp.exp(w_ref[...])
    kkr = k * kk_ref[...]
    kk = kkr / jnp.maximum(jnp.sqrt(jnp.sum(kkr * kkr, axis=-1, keepdims=True)), 1e-12)
    km = k * (1.0 + (asig - 1.0) * ka_ref[...])
    a = -kk
    b = kk * asig

    ti = lax.broadcasted_iota(jnp.int32, (C, C), 0)
    si = lax.broadcasted_iota(jnp.int32, (C, C), 1)
    strict = (si < ti)[None]
    incl = (si <= ti)[None]
    tri = jnp.broadcast_to((si <= ti).astype(F32)[None], (hb, C, C))
    cum = _bdot(tri, lw, _NN, precision=HIGHEST)
    cum_last = cum[:, C - 1:C, :]
    e_pos = jnp.exp(cum)
    e_neg = jnp.exp(-cum)
    e_tail = jnp.exp(cum_last - cum)
    at = a * jnp.exp(cum - lw)
    rt = r * e_pos
    bt = b * e_neg
    kt = km * e_neg
    bh = b * e_tail
    kh = km * e_tail

    a_ab = jnp.where(strict, _bdot(at, bt, _NT), 0.0)
    a_ak = jnp.where(strict, _bdot(at, kt, _NT), 0.0)
    a_rb = jnp.where(incl, _bdot(rt, bt, _NT), 0.0)
    a_rk = jnp.where(incl, _bdot(rt, kt, _NT), 0.0)

    x = jnp.concatenate([at, _bdot(a_ak, v, _NN)], axis=-1)
    p = a_ab
    n_doublings = max(1, (C - 1).bit_length())
    for i in range(n_doublings):
        x = x + _bdot(p, x, _NN)
        if i + 1 < n_doublings:
            p = _bdot(p, p, _NN)
    wa = x[..., :N]
    uv = x[..., N:]

    s0 = s_ref[...]
    u = _bdot(wa, s0, _NT) + uv
    y = _bdot(rt, s0, _NT) + _bdot(a_rb, u, _NN) + _bdot(a_rk, v, _NN)
    s_ref[...] = s0 * jnp.exp(cum_last) + _bdot(u, bh, _TN) + _bdot(v, kh, _TN)

    mu = jnp.mean(y, axis=-1, keepdims=True)
    yc = y - mu
    var = jnp.mean(yc * yc, axis=-1, keepdims=True)
    yn = yc * lax.rsqrt(var + RWKV_GN_EPS) * gw_ref[...] + gb_ref[...]
    bonus = jnp.sum(r * km * rk_ref[...], axis=-1, keepdims=True) * v
    o_ref[...] = yn + bonus


def _rwkv_scan(r, w_raw, k, v, a_sig, k_k, k_a, r_k, lnx_w, lnx_b, n_heads, hb=8):
    BH, S, N = r.shape
    C = RWKV_CHUNK
    hb = min(hb, n_heads)
    assert n_heads % hb == 0 and S % C == 0
    groups_per_batch = n_heads // hb
    seq = pl.BlockSpec((hb, C, N), lambda g, c: (g, c, 0))
    par = pl.BlockSpec((hb, 1, N), lambda g, c: (g % groups_per_batch, 0, 0))
    vmem = 2 * 6 * hb * C * 128 * 4 + 64 * hb * C * 128 * 4
    return pl.pallas_call(
        _rwkv_scan_kernel,
        grid=(BH // hb, S // C),
        in_specs=[seq] * 5 + [par] * 5,
        out_specs=seq,
        out_shape=jax.ShapeDtypeStruct((BH, S, N), F32),
        scratch_shapes=[pltpu.VMEM((hb, N, N), F32)],
        compiler_params=_compiler_params(("parallel", "arbitrary"), vmem),
        name="rwkv7_scan",
    )(r, w_raw, k, v, a_sig, k_k, k_a, r_k, lnx_w, lnx_b)


def _softplus(x):
    return jnp.maximum(x, 0.0) + jnp.log(1.0 + jnp.exp(-jnp.abs(x)))


def _rwkv7_mix(x, v_first, B, S, mix, w_r, w_k, w_v, w_o, w0, w1, w2, a0, a1, a2, g1, g2,
               k_k, k_a, r_k, lnx_w, lnx_b, vres):
    T, D = x.shape
    N = RWKV_HEAD
    H = D // N
    x3 = x.reshape(B, S, D)
    xx = jnp.pad(x3[:, :-1], ((0, 0), (1, 0), (0, 0))) - x3
    xr, xw, xk, xv, xa, xg = [(x3 + xx * mix[i]).astype(BF16).reshape(T, D) for i in range(6)]
    bf = lambda t: t.astype(BF16)
    r = _matmul(xr, bf(w_r), F32)
    k = _matmul(xk, bf(w_k), F32)
    v = _matmul(xv, bf(w_v), F32)
    w_raw = -_softplus(-(w0 + _matmul(bf(jnp.tanh(_matmul(xw, bf(w1), F32))), bf(w2), F32))) - 0.5
    a = jax.nn.sigmoid(a0 + _matmul(bf(_matmul(xa, bf(a1), F32)), bf(a2), F32))
    g = _matmul(bf(jax.nn.sigmoid(_matmul(xg, bf(g1), F32))), bf(g2), F32)
    if vres is None:
        v_first = v
    else:
        v0, v1, v2 = vres
        v = v + (v_first - v) * jax.nn.sigmoid(v0 + _matmul(bf(_matmul(xv, bf(v1), F32)), bf(v2), F32))

    heads = lambda t: t.reshape(B, S, H, N).transpose(0, 2, 1, 3).reshape(B * H, S, N)
    par = lambda p: p.reshape(H, 1, N)
    o = _rwkv_scan(heads(r), heads(w_raw), heads(k), heads(v), heads(a),
                   par(k_k), par(k_a), par(r_k), par(lnx_w), par(lnx_b), H)
    o = o.reshape(B, H, S, N).transpose(0, 2, 1, 3).reshape(T, D)
    return _matmul(bf(o * g), bf(w_o), F32), v_first


def _gla_kernel(q_ref, k_ref, v_ref, g_ref, gkl_ref, w2_ref, gb_ref, nw_ref, o_ref, s_ref):
    @pl.when(pl.program_id(2) == 0)
    def _():
        s_ref[...] = jnp.zeros_like(s_ref)

    C = GLA_CHUNK
    ts, dk = q_ref.shape
    gk = jnp.dot(gkl_ref[...], w2_ref[...], precision=HIGHEST, preferred_element_type=F32) + gb_ref[...]
    gk = (jnp.minimum(gk, 0.0) - jnp.log(1.0 + jnp.exp(-jnp.abs(gk)))) / GLA_GATE_NORM
    ti = lax.broadcasted_iota(jnp.int32, (C, C), 0)
    si = lax.broadcasted_iota(jnp.int32, (C, C), 1)
    incl = si <= ti
    tri = incl.astype(F32)
    nt = (((1,), (1,)), ((), ()))
    tn = (((0,), (0,)), ((), ()))
    for c in range(ts // C):
        rows = slice(c * C, (c + 1) * C)
        bcum = jnp.dot(tri, gk[rows], precision=HIGHEST, preferred_element_type=F32)
        b_last = bcum[C - 1:C, :]
        q = q_ref[rows, :].astype(F32) * (dk ** -0.5)
        k = k_ref[rows, :].astype(F32)
        v = v_ref[rows, :]
        qe = (q * jnp.exp(bcum)).astype(BF16)
        ke = (k * jnp.exp(-bcum)).astype(BF16)
        att = jnp.where(incl, lax.dot_general(qe, ke, nt, preferred_element_type=F32), 0.0)
        s0 = s_ref[...]
        o = (jnp.dot(att.astype(BF16), v, preferred_element_type=F32)
             + lax.dot_general(qe, s0.astype(BF16), nt, preferred_element_type=F32))
        kd = (k * jnp.exp(b_last - bcum)).astype(BF16)
        s_ref[...] = s0 * jnp.exp(b_last) + lax.dot_general(v, kd, tn, preferred_element_type=F32)
        on = o * lax.rsqrt(jnp.mean(o * o, axis=-1, keepdims=True) + GLA_RMS_EPS) * nw_ref[...]
        g = g_ref[rows, :].astype(F32)
        o_ref[rows, :] = (on * (g * jax.nn.sigmoid(g))).astype(o_ref.dtype)


def _gla_mix(x_bf16, B, S, w_in, gk_w2, gk_b, norm_w, w_o, ts=256):
    T, D = x_bf16.shape
    rank, dk_tot = gk_w2.shape
    dv_tot = D
    dv = GLA_HEAD_DV
    H = dv_tot // dv
    dk = dk_tot // H
    main = 2 * dk_tot + 2 * dv_tot
    assert w_in.shape[1] == main + rank and dv % dk == 0 and (2 * dk_tot) % dv == 0
    proj = _matmul(x_bf16, w_in[:, :main].astype(BF16), BF16)
    gk_lr = _matmul(x_bf16, w_in[:, main:].astype(BF16), F32)
    ts = min(ts, S)
    assert S % ts == 0 and ts % GLA_CHUNK == 0
    nsb = S // ts
    k_off = dk_tot // dk
    v_off = 2 * dk_tot // dv
    g_off = (2 * dk_tot + dv_tot) // dv
    row = lambda b, h, s: b * nsb + s
    vmem = 2 * (2 * ts * dk * 2 + 3 * ts * dv * 2 + ts * 128 * 4) + 8 * ts * dk * 4 + dv * dk * 4
    o = pl.pallas_call(
        _gla_kernel,
        grid=(B, H, nsb),
        in_specs=[pl.BlockSpec((ts, dk), lambda b, h, s: (row(b, h, s), h)),
                  pl.BlockSpec((ts, dk), lambda b, h, s: (row(b, h, s), k_off + h)),
                  pl.BlockSpec((ts, dv), lambda b, h, s: (row(b, h, s), v_off + h)),
                  pl.BlockSpec((ts, dv), lambda b, h, s: (row(b, h, s), g_off + h)),
                  pl.BlockSpec((ts, rank), lambda b, h, s: (row(b, h, s), 0)),
                  pl.BlockSpec((rank, dk), lambda b, h, s: (0, h)),
                  pl.BlockSpec((1, dk), lambda b, h, s: (0, h)),
                  pl.BlockSpec((1, dv), lambda b, h, s: (0, 0))],
        out_specs=pl.BlockSpec((ts, dv), lambda b, h, s: (row(b, h, s), h)),
        out_shape=jax.ShapeDtypeStruct((T, dv_tot), BF16),
        scratch_shapes=[pltpu.VMEM((dv, dk), F32)],
        compiler_params=_compiler_params(("parallel", "parallel", "arbitrary"), vmem),
        name="gla_chunk_scan",
    )(proj, proj, proj, proj, gk_lr, gk_w2, gk_b.reshape(1, dk_tot), norm_w.reshape(1, dv))
    return _matmul(o, w_o.astype(BF16), F32)


def kernel(x, rwkv_mix, rwkv_w_r, rwkv_w_k, rwkv_w_v, rwkv_w_o, rwkv_w0, rwkv_w1, rwkv_w2, rwkv_a0, rwkv_a1, rwkv_a2, rwkv_v0, rwkv_v1, rwkv_v2, rwkv_g1, rwkv_g2, rwkv_k_k, rwkv_k_a, rwkv_r_k, rwkv_lnx_w, rwkv_lnx_b, gla_w_in, gla_gk_w2, gla_gk_b, gla_norm_w, gla_w_o, ln_g, ln_b, moe_w_gate, moe_w_up, moe_w_down, router_w, router_bias):
    B, S, D = x.shape
    depth = ln_g.shape[0]
    alpha = (2.0 * depth) ** 0.25
    xf = x.reshape(B * S, D)
    xb = xf.astype(BF16)
    v_first = None
    for i in range(depth):
        j = i // 2
        if i % 2 == 0:
            vres = None if j == 0 else (rwkv_v0[j - 1], rwkv_v1[j - 1], rwkv_v2[j - 1])
            y, v_first = _rwkv7_mix(
                xf, v_first, B, S, rwkv_mix[j], rwkv_w_r[j], rwkv_w_k[j], rwkv_w_v[j], rwkv_w_o[j],
                rwkv_w0[j], rwkv_w1[j], rwkv_w2[j], rwkv_a0[j], rwkv_a1[j], rwkv_a2[j],
                rwkv_g1[j], rwkv_g2[j], rwkv_k_k[j], rwkv_k_a[j], rwkv_r_k[j],
                rwkv_lnx_w[j], rwkv_lnx_b[j], vres)
        else:
            y = _gla_mix(xb, B, S, gla_w_in[j], gla_gk_w2[j], gla_gk_b[j], gla_norm_w[j], gla_w_o[j])
        xf, xb = _add_layernorm(xf, y, ln_g[i, 0], ln_b[i, 0], alpha)
        m = _grouped_moe(xf, xb, router_w, router_bias, moe_w_gate[i], moe_w_up[i], moe_w_down[i])
        xf, xb = _add_layernorm(xf, m, ln_g[i, 1], ln_b[i, 1], alpha)
    return xf.reshape(B, S, D)
```
